```python
import math
import jax, jax.numpy as jnp
from jax import lax
import numpy as np

D_MODEL = 1024
BATCH = 2
SEQ = 16384
DEPTH = 2

EPS = 1e-6
N_MEM = 256
HEAD_DIM = 64
X_HEADS = 4
X_WIDTH = X_HEADS * HEAD_DIM
MIX_WIDTH = D_MODEL
TM_WIDTH = MIX_WIDTH - X_WIDTH
LRU_WIDTH = TM_WIDTH
LRU_BLOCKS = 4
LRU_BLOCK = LRU_WIDTH // LRU_BLOCKS
CONV_WIDTH = 4
LRU_C = 8.0
LRU_IN = 2 * LRU_WIDTH + X_WIDTH
DSA_HEADS = TM_WIDTH // HEAD_DIM
DSA_V_DIM = 128
IDX_HEADS = 4
IDX_DIM = 64
TOPK_MAX = 256
Q_BLOCK = 128
DSA_SIZES = (DSA_HEADS * HEAD_DIM, HEAD_DIM, DSA_V_DIM, IDX_HEADS * IDX_DIM, IDX_DIM, IDX_HEADS, X_WIDTH)
DSA_IN = int(sum(DSA_SIZES))
DSA_SPLITS = tuple(int(c) for c in np.cumsum(DSA_SIZES)[:-1])
REL_BUCKETS = 32
REL_MAX_DIST = 128
N_GROUPS = 4
EXPERTS_PER_GROUP = 8
N_EXPERTS = N_GROUPS * EXPERTS_PER_GROUP
TOP_K_IN_GROUP = 2
D_EXPERT = 512
MOE_BLOCK = 128
N_LRU_LAYERS = (DEPTH + 1) // 2
N_DSA_LAYERS = DEPTH // 2

kernel_name = "hybrid_rglru_dsa_memxattn_hmoe"


def rms_norm(x, g):
    xf = x.astype(jnp.float32)
    y = xf * lax.rsqrt(jnp.mean(xf * xf, axis=-1, keepdims=True) + EPS)
    return (y * g.astype(jnp.float32)).astype(x.dtype)


def rms_unit(x):
    xf = x.astype(jnp.float32)
    return (xf * lax.rsqrt(jnp.mean(xf * xf, axis=-1, keepdims=True) + EPS)).astype(x.dtype)


def t5_bucket(dist):
    max_exact = REL_BUCKETS // 2
    d = jnp.maximum(dist, 0)
    df = jnp.maximum(d, 1).astype(jnp.float32)
    large = max_exact + (jnp.log(df / max_exact) / math.log(REL_MAX_DIST / max_exact)
                         * (REL_BUCKETS - max_exact)).astype(jnp.int32)
    large = jnp.minimum(large, REL_BUCKETS - 1)
    return jnp.where(d < max_exact, d, large)


def memory_kv(mem, g, w):
    B, M, _ = mem.shape
    k, v = jnp.split(rms_norm(mem, g) @ w, 2, axis=-1)
    return k.reshape(B, M, X_HEADS, HEAD_DIM), v.reshape(B, M, X_HEADS, HEAD_DIM)


def memory_cross_attention(xq, mk, mv, gq, gk):
    B, S, _ = xq.shape
    q = rms_norm(xq.reshape(B, S, X_HEADS, HEAD_DIM), gq)
    k = rms_norm(mk, gk)
    logits = jnp.einsum('bshd,bmhd->bhsm', q, k).astype(jnp.float32) * HEAD_DIM ** -0.5
    p = jax.nn.softmax(logits, axis=-1).astype(mv.dtype)
    return jnp.einsum('bhsm,bmhd->bshd', p, mv).reshape(B, S, X_WIDTH)


def causal_depthwise_conv(x, w, b):
    S = x.shape[1]
    xp = jnp.pad(x, ((0, 0), (CONV_WIDTH - 1, 0), (0, 0)))
    out = b
    for k in range(CONV_WIDTH):
        out = out + xp[:, k:k + S] * w[k]
    return out


def rglru(x, w_a, b_a, w_x, b_x, lam):
    B, S, _ = x.shape
    xf = x.astype(jnp.float32)
    xb = xf.reshape(B, S, LRU_BLOCKS, LRU_BLOCK)
    r = jax.nn.sigmoid(jnp.einsum('bsni,nij->bsnj', xb, w_a.astype(jnp.float32)).reshape(B, S, LRU_WIDTH) + b_a)
    i = jax.nn.sigmoid(jnp.einsum('bsni,nij->bsnj', xb, w_x.astype(jnp.float32)).reshape(B, S, LRU_WIDTH) + b_x)
    log_a = -LRU_C * r * jax.nn.softplus(-lam.astype(jnp.float32))
    a = jnp.exp(log_a)
    u = jnp.sqrt(-jnp.expm1(2.0 * log_a)) * (i * xf)

    def combine(left, right):
        a1, b1 = left
        a2, b2 = right
        return a1 * a2, a2 * b1 + b2

    _, h = lax.associative_scan(combine, (a, u), axis=1)
    return h.astype(x.dtype)


def rglru_mixer(hn, mk, mv, w_in, conv_w, conv_b, w_a, b_a, w_x, b_x, lam, xq_g, xk_g):
    proj = hn @ w_in
    xs, gate, xq = jnp.split(proj, [LRU_WIDTH, 2 * LRU_WIDTH], axis=-1)
    xs = causal_depthwise_conv(xs, conv_w, conv_b)
    y = rglru(xs, w_a, b_a, w_x, b_x, lam) * jax.nn.gelu(gate)
    xo = memory_cross_attention(xq, mk, mv, xq_g, xk_g)
    return jnp.concatenate([y, xo], axis=-1)


def dsa_mixer(hn, mk, mv, w_in, q_g, k_g, w_uv, rel_bias, xq_g, xk_g):
    B, S, _ = hn.shape
    proj = hn @ w_in
    q, k, v, qi, ki, wi, xq = jnp.split(proj, DSA_SPLITS, axis=-1)
    q = rms_norm(q.reshape(B, S, DSA_HEADS, HEAD_DIM), q_g)
    k = rms_norm(k, k_g)
    qi = qi.reshape(B, S, IDX_HEADS, IDX_DIM)
    ki = rms_unit(ki)
    wi = wi * IDX_HEADS ** -0.5
    topk = min(TOPK_MAX, S // 4)
    n_blk = S // Q_BLOCK
    key_pos = jnp.arange(S, dtype=jnp.int32)

    def to_blocks(a):
        return jnp.swapaxes(a.reshape(B, n_blk, Q_BLOCK, *a.shape[2:]), 0, 1)

    def block_fn(args):
        qb, qib, wib, blk = args
        t = blk * Q_BLOCK + jnp.arange(Q_BLOCK, dtype=jnp.int32)
        dots = jnp.einsum('bqhd,bsd->bqhs', qib, ki).astype(jnp.float32) * IDX_DIM ** -0.5
        score = jnp.einsum('bqh,bqhs->bqs', wib.astype(jnp.float32), jax.nn.relu(dots))
        score = jnp.where((key_pos[None, :] <= t[:, None])[None], score, -jnp.inf)
        _, sel = lax.top_k(score, topk)
        k_sel = jax.vmap(lambda kk, ii: kk[ii])(k, sel)
        v_sel = jax.vmap(lambda vv, ii: vv[ii])(v, sel)
        dist = t[None, :, None] - sel
        bias = jnp.swapaxes(rel_bias[t5_bucket(dist)], -1, -2)
        logits = jnp.einsum('bqhd,bqkd->bqhk', qb, k_sel).astype(jnp.float32) * HEAD_DIM ** -0.5
        logits = logits + bias.astype(jnp.float32)
        logits = jnp.where((dist >= 0)[:, :, None, :], logits, -jnp.inf)
        p = jax.nn.softmax(logits, axis=-1).astype(v.dtype)
        o = jnp.einsum('bqhk,bqkc->bqhc', p, v_sel)
        return jnp.einsum('bqhc,hcd->bqhd', o, w_uv).reshape(B, Q_BLOCK, DSA_HEADS * HEAD_DIM)

    out = lax.map(block_fn, (to_blocks(q), to_blocks(qi), to_blocks(wi), jnp.arange(n_blk, dtype=jnp.int32)))
    y = jnp.swapaxes(out, 0, 1).reshape(B, S, DSA_HEADS * HEAD_DIM)
    xo = memory_cross_attention(xq, mk, mv, xq_g, xk_g)
    return jnp.concatenate([y, xo], axis=-1)


def grouped_swiglu(xt, eid, w_gate, w_up, w_down):
    N, D = xt.shape
    A = eid.shape[0]
    counts = jnp.bincount(eid, length=N_EXPERTS)
    padded = (counts + MOE_BLOCK - 1) // MOE_BLOCK * MOE_BLOCK
    seg_end = jnp.cumsum(padded)
    pad_start = seg_end - padded
    start = jnp.cumsum(counts) - counts
    order = jnp.argsort(eid)
    eid_sorted = eid[order]
    dest_sorted = (pad_start[eid_sorted] + jnp.arange(A) - start[eid_sorted]).astype(jnp.int32)
    dest = jnp.zeros((A,), jnp.int32).at[order].set(dest_sorted)
    n_blocks = -(-A // MOE_BLOCK) + N_EXPERTS
    P = n_blocks * MOE_BLOCK
    tok = jnp.arange(A, dtype=jnp.int32) // TOP_K_IN_GROUP
    buf = jnp.zeros((P, D), xt.dtype).at[dest].set(xt[tok])
    block_eid = jnp.minimum(jnp.searchsorted(seg_end, jnp.arange(n_blocks) * MOE_BLOCK, side='right'),
                            N_EXPERTS - 1)

    def expert_block(args):
        xb, e = args
        return (jax.nn.silu(xb @ w_gate[e]) * (xb @ w_up[e])) @ w_down[e]

    out = lax.map(expert_block, (buf.reshape(n_blocks, MOE_BLOCK, D), block_eid))
    return out.reshape(P, D)[dest]


def hier_moe(h, w_grp, b_grp, w_rt, b_rt, w_gate, w_up, w_down):
    B, S, D = h.shape
    xt = h.reshape(B * S, D)
    N = xt.shape[0]
    p_grp = jax.nn.softmax((xt @ w_grp).astype(jnp.float32) + b_grp, axis=-1)
    p_g, g_sel = lax.top_k(p_grp, 1)
    e_logits = ((xt @ w_rt).astype(jnp.float32) + b_rt).reshape(N, N_GROUPS, EXPERTS_PER_GROUP)
    e_logits = jnp.take_along_axis(e_logits, g_sel[:, :, None], axis=1)[:, 0]
    p_e = jax.nn.softmax(e_logits, axis=-1)
    v2, e_sel = lax.top_k(p_e, TOP_K_IN_GROUP)
    gate = p_g * v2 / jnp.sum(v2, axis=-1, keepdims=True)
    eid = (g_sel * EXPERTS_PER_GROUP + e_sel).reshape(-1).astype(jnp.int32)
    y = grouped_swiglu(xt, eid, w_gate, w_up, w_down).reshape(N, TOP_K_IN_GROUP, D)
    out = jnp.sum(y * gate[..., None].astype(y.dtype), axis=1)
    return out.reshape(B, S, D)


def setup_inputs(seed: int = 0) -> dict:
    key = jax.random.key(seed)
    ks = iter(jax.random.split(key, 40))
    f32 = jnp.float32

    def nrm(shape, scale):
        return jax.random.normal(next(ks), shape, f32) * scale

    def gain(shape):
        return 1.0 + nrm(shape, 0.02)

    NL, ND = N_LRU_LAYERS, N_DSA_LAYERS
    x = nrm((BATCH, SEQ, D_MODEL), 1.0)
    mem = nrm((BATCH, N_MEM, D_MODEL), 1.0)
    norm_mix = gain((DEPTH, D_MODEL))
    norm_ffn = gain((DEPTH, D_MODEL))
    mem_norm = gain((DEPTH, D_MODEL))
    w_mem_kv = nrm((DEPTH, D_MODEL, 2 * X_WIDTH), D_MODEL ** -0.5)
    xq_norm = gain((DEPTH, HEAD_DIM))
    xk_norm = gain((DEPTH, HEAD_DIM))
    w_out = nrm((DEPTH, MIX_WIDTH, D_MODEL), MIX_WIDTH ** -0.5)
    lru_w_in = nrm((NL, D_MODEL, LRU_IN), D_MODEL ** -0.5)
    lru_conv_w = nrm((NL, CONV_WIDTH, LRU_WIDTH), CONV_WIDTH ** -0.5)
    lru_conv_b = nrm((NL, LRU_WIDTH), 0.01)
    lru_w_a = nrm((NL, LRU_BLOCKS, LRU_BLOCK, LRU_BLOCK), LRU_BLOCK ** -0.5)
    lru_b_a = nrm((NL, LRU_WIDTH), 0.01)
    lru_w_x = nrm((NL, LRU_BLOCKS, LRU_BLOCK, LRU_BLOCK), LRU_BLOCK ** -0.5)
    lru_b_x = nrm((NL, LRU_WIDTH), 0.01)
    a0 = jax.random.uniform(next(ks), (NL, LRU_WIDTH), f32, 0.9, 0.999)
    p = a0 ** (1.0 / LRU_C)
    lru_lambda = jnp.log(p) - jnp.log1p(-p)
    dsa_w_in = nrm((ND, D_MODEL, DSA_IN), D_MODEL ** -0.5)
    dsa_q_norm = gain((ND, HEAD_DIM))
    dsa_k_norm = gain((ND, HEAD_DIM))
    dsa_w_uv = nrm((ND, DSA_HEADS, DSA_V_DIM, HEAD_DIM), DSA_V_DIM ** -0.5)
    rel_bias = nrm((REL_BUCKETS, DSA_HEADS), 0.5)
    moe_w_group = nrm((DEPTH, D_MODEL, N_GROUPS), D_MODEL ** -0.5)
    moe_b_group = nrm((DEPTH, N_GROUPS), 0.01)
    moe_w_router = nrm((DEPTH, D_MODEL, N_EXPERTS), D_MODEL ** -0.5)
    moe_b_router = nrm((DEPTH, N_EXPERTS), 0.01)
    moe_w_gate = nrm((DEPTH, N_EXPERTS, D_MODEL, D_EXPERT), D_MODEL ** -0.5)
    moe_w_up = nrm((DEPTH, N_EXPERTS, D_MODEL, D_EXPERT), D_MODEL ** -0.5)
    moe_w_down = nrm((DEPTH, N_EXPERTS, D_EXPERT, D_MODEL), D_EXPERT ** -0.5)
    return {"x": x, "mem": mem, "norm_mix": norm_mix, "norm_ffn": norm_ffn, "mem_norm": mem_norm,
            "w_mem_kv": w_mem_kv, "xq_norm": xq_norm, "xk_norm": xk_norm, "w_out": w_out,
            "lru_w_in": lru_w_in, "lru_conv_w": lru_conv_w, "lru_conv_b": lru_conv_b,
            "lru_w_a": lru_w_a, "lru_b_a": lru_b_a, "lru_w_x": lru_w_x, "lru_b_x": lru_b_x,
            "lru_lambda": lru_lambda, "dsa_w_in": dsa_w_in, "dsa_q_norm": dsa_q_norm,
            "dsa_k_norm": dsa_k_norm, "dsa_w_uv": dsa_w_uv, "rel_bias": rel_bias,
            "moe_w_group": moe_w_group, "moe_b_group": moe_b_group, "moe_w_router": moe_w_router,
            "moe_b_router": moe_b_router, "moe_w_gate": moe_w_gate, "moe_w_up": moe_w_up,
            "moe_w_down": moe_w_down}


def reference(x, mem, norm_mix, norm_ffn, mem_norm, w_mem_kv, xq_norm, xk_norm, w_out,
              lru_w_in, lru_conv_w, lru_conv_b, lru_w_a, lru_b_a, lru_w_x, lru_b_x, lru_lambda,
              dsa_w_in, dsa_q_norm, dsa_k_norm, dsa_w_uv, rel_bias,
              moe_w_group, moe_b_group, moe_w_router, moe_b_router, moe_w_gate, moe_w_up, moe_w_down):
    for i in range(DEPTH):
        j = i // 2
        mk, mv = memory_kv(mem, mem_norm[i], w_mem_kv[i])
        hn = rms_norm(x, norm_mix[i])
        if i % 2 == 0:
            mixed = rglru_mixer(hn, mk, mv, lru_w_in[j], lru_conv_w[j], lru_conv_b[j], lru_w_a[j],
                                lru_b_a[j], lru_w_x[j], lru_b_x[j], lru_lambda[j], xq_norm[i], xk_norm[i])
        else:
            mixed = dsa_mixer(hn, mk, mv, dsa_w_in[j], dsa_q_norm[j], dsa_k_norm[j], dsa_w_uv[j],
                              rel_bias, xq_norm[i], xk_norm[i])
        x = x + mixed @ w_out[i]
        x = x + hier_moe(rms_norm(x, norm_ffn[i]), moe_w_group[i], moe_b_group[i], moe_w_router[i],
                         moe_b_router[i], moe_w_gate[i], moe_w_up[i], moe_w_down[i])
    return x
```

```python
import functools
import math

import numpy as np
import jax
import jax.numpy as jnp
from jax import lax
from jax.experimental import pallas as pl
from jax.experimental.pallas import tpu as pltpu

EPS = 1e-6
HEAD_DIM = 64
X_HEADS = 4
X_WIDTH = X_HEADS * HEAD_DIM
LRU_BLOCKS = 4
CONV_WIDTH = 4
LRU_C = 8.0
DSA_V_DIM = 128
IDX_HEADS = 4
IDX_DIM = 64
TOPK_MAX = 256
REL_BUCKETS = 32
REL_MAX_DIST = 128
N_GROUPS = 4
EXPERTS_PER_GROUP = 8
TOP_K_IN_GROUP = 2

LANES = 128
SUBLANES = 8
VMEM_LIMIT = 56 * 1024 * 1024
TOKEN_TILE = 512
LRU_TILE = 256
Q_TILE = 128
KEY_CHUNK = 256
MOE_ROWS = 256
ROUTER_ROWS = 40
MASK_NEG = -1e30
INT_MIN = -(2 ** 31)
F32 = jnp.float32
BF16 = jnp.bfloat16
HIGHEST = lax.Precision.HIGHEST


def _cparams(*sem):
    return pltpu.CompilerParams(dimension_semantics=sem, vmem_limit_bytes=VMEM_LIMIT)


def _full(shape):
    n = len(shape)
    return pl.BlockSpec(shape, lambda *_: (0,) * n)


def _rms_rows(x, g):
    return x * lax.rsqrt(jnp.mean(x * x, axis=-1, keepdims=True) + EPS) * g


def _head_ones(width, head):
    idx = np.arange(width) // head
    return jnp.asarray((idx[:, None] == idx[None, :]).astype(np.float32) / head)


def _dot(a, b):
    return jnp.dot(a, b, preferred_element_type=F32)


def _dot_exact(a, b):
    return jnp.dot(a, b, preferred_element_type=F32, precision=HIGHEST)


def _memkv_kernel(mem_ref, g_ref, w_ref, gk_ref, j_ref, k_ref, v_ref):
    y = _rms_rows(mem_ref[0], g_ref[...])
    kv = _dot(y.astype(BF16), w_ref[...])
    k = kv[:, :X_WIDTH]
    msq = _dot_exact(k * k, j_ref[...])
    k_ref[0] = k * lax.rsqrt(msq + EPS) * gk_ref[...]
    v_ref[0] = kv[:, X_WIDTH:]


def _memory_kv(mem, g, w, gk):
    B, M, D = mem.shape
    gk_t = jnp.tile(gk, X_HEADS)[None, :]
    k, v = pl.pallas_call(
        _memkv_kernel,
        grid=(B,),
        in_specs=[pl.BlockSpec((1, M, D), lambda b: (b, 0, 0)), _full((1, D)), _full((D, 2 * X_WIDTH)),
                  _full((1, X_WIDTH)), _full((X_WIDTH, X_WIDTH))],
        out_specs=[pl.BlockSpec((1, M, X_WIDTH), lambda b: (b, 0, 0))] * 2,
        out_shape=[jax.ShapeDtypeStruct((B, M, X_WIDTH), F32)] * 2,
        compiler_params=_cparams("arbitrary"),
        name="memory_kv",
    )(mem, g[None, :], w.astype(BF16), gk_t, _head_ones(X_WIDTH, HEAD_DIM))
    eye = jnp.eye(X_HEADS, dtype=F32)
    k4 = k.reshape(B, M, X_HEADS, HEAD_DIM)
    v4 = v.reshape(B, M, X_HEADS, HEAD_DIM)
    kbd = jnp.einsum('bmhd,hg->bhdgm', k4, eye).reshape(B, X_WIDTH, X_HEADS * M)
    vbd = jnp.einsum('bmhd,hg->bgmhd', v4, eye).reshape(B, X_HEADS * M, X_WIDTH)
    return kbd.astype(BF16), vbd.astype(BF16)


def _lru_in_kernel(x_ref, g_ref, w_ref, xs_ref, gate_ref, xq_ref):
    hn = _rms_rows(x_ref[...], g_ref[...]).astype(BF16)
    proj = _dot(hn, w_ref[...])
    wd = xs_ref.shape[1]
    xs_ref[...] = proj[:, :wd]
    gate_ref[...] = proj[:, wd:2 * wd]
    xq_ref[...] = proj[:, 2 * wd:]


def _lru_in(x2, g, w):
    N, D = x2.shape
    wd = (w.shape[1] - X_WIDTH) // 2
    tm = min(TOKEN_TILE, N)
    row = lambda c: pl.BlockSpec((tm, c), lambda i: (i, 0))
    return pl.pallas_call(
        _lru_in_kernel,
        grid=(N // tm,),
        in_specs=[row(D), _full((1, D)), _full(w.shape)],
        out_specs=[row(wd), row(wd), row(X_WIDTH)],
        out_shape=[jax.ShapeDtypeStruct((N, wd), F32), jax.ShapeDtypeStruct((N, wd), F32),
                   jax.ShapeDtypeStruct((N, X_WIDTH), F32)],
        compiler_params=_cparams("parallel"),
        name="lru_in_proj",
    )(x2, g[None, :], w.astype(BF16))


def _gelu_tanh(x):
    c = math.sqrt(2.0 / math.pi)
    return 0.5 * x * (1.0 + jnp.tanh(c * (x + 0.044715 * (x * x * x))))


def _softplus(x):
    return jnp.maximum(x, 0.0) + jnp.log(1.0 + jnp.exp(-jnp.abs(x)))


def _lru_kernel(xs_ref, gate_ref, cw_ref, cb_ref, wa_ref, ba_ref, wx_ref, bx_ref, lam_ref,
                y_ref, tail_ref, h_ref, a_scr, u_scr):
    ts, wd = xs_ref.shape[1], xs_ref.shape[2]

    @pl.when(pl.program_id(1) == 0)
    def _():
        tail_ref[...] = jnp.zeros_like(tail_ref)
        h_ref[...] = jnp.zeros_like(h_ref)

    x = xs_ref[0]
    ext = jnp.concatenate([tail_ref[...], x], axis=0)
    tail_ref[...] = x[ts - SUBLANES:, :]
    conv = cb_ref[...]
    for k in range(CONV_WIDTH):
        d = CONV_WIDTH - 1 - k
        conv = conv + ext[SUBLANES - d:SUBLANES - d + ts, :] * cw_ref[k:k + 1, :]
    cb16 = conv.astype(BF16)
    r = jax.nn.sigmoid(_dot(cb16, wa_ref[...]) + ba_ref[...])
    gi = jax.nn.sigmoid(_dot(cb16, wx_ref[...]) + bx_ref[...])
    log_a = (-LRU_C) * r * _softplus(-lam_ref[...])
    a = jnp.exp(log_a)
    u = jnp.sqrt(1.0 - jnp.exp(2.0 * log_a)) * (gi * conv)

    rowi = lax.broadcasted_iota(jnp.int32, (ts, wd), 0) % SUBLANES
    for d in (1, 2, 4):
        keep = rowi >= d
        a_sh = jnp.where(keep, pltpu.roll(a, d, 0), 1.0)
        u_sh = jnp.where(keep, pltpu.roll(u, d, 0), 0.0)
        u = a * u_sh + u
        a = a * a_sh
    a_scr[...] = a
    u_scr[...] = u

    h = h_ref[...]
    for gidx in range(ts // SUBLANES):
        sl = slice(gidx * SUBLANES, (gidx + 1) * SUBLANES)
        hg = u_scr[sl, :] + a_scr[sl, :] * h
        u_scr[sl, :] = hg
        h = jnp.broadcast_to(hg[SUBLANES - 1:SUBLANES, :], (SUBLANES, wd))
    h_ref[...] = h
    y_ref[0] = u_scr[...] * _gelu_tanh(gate_ref[0])


def _block_diag(w):
    nb, bi, bo = w.shape
    eye = jnp.eye(nb, dtype=w.dtype)
    return jnp.einsum('nij,nm->nimj', w, eye).reshape(nb * bi, nb * bo)


def _lru_core(xs, gate, conv_w, conv_b, w_a, b_a, w_x, b_x, lam):
    B, S, wd = xs.shape
    ts = min(LRU_TILE, S)
    blk = pl.BlockSpec((1, ts, wd), lambda b, i: (b, i, 0))
    vec = _full((1, wd))
    return pl.pallas_call(
        _lru_kernel,
        grid=(B, S // ts),
        in_specs=[blk, blk, _full((CONV_WIDTH, wd)), vec, _full((wd, wd)), vec, _full((wd, wd)), vec, vec],
        out_specs=blk,
        out_shape=jax.ShapeDtypeStruct((B, S, wd), F32),
        scratch_shapes=[pltpu.VMEM((SUBLANES, wd), F32), pltpu.VMEM((SUBLANES, wd), F32),
                        pltpu.VMEM((ts, wd), F32), pltpu.VMEM((ts, wd), F32)],
        compiler_params=_cparams("arbitrary", "arbitrary"),
        name="rglru_core",
    )(xs, gate, conv_w, conv_b[None, :], _block_diag(w_a).astype(BF16), b_a[None, :],
      _block_diag(w_x).astype(BF16), b_x[None, :], lam[None, :])


def _dsa_in_kernel(x_ref, g_ref, w_ref, gq_ref, gkk_ref, jq_ref, jk_ref,
                   q_ref, kk_ref, v_ref, qi_ref, xq_ref, wi_ref):
    hn = _rms_rows(x_ref[...], g_ref[...]).astype(BF16)
    proj = _dot(hn, w_ref[...])
    nq = q_ref.shape[1]
    q = proj[:, :nq]
    q = q * lax.rsqrt(_dot_exact(q * q, jq_ref[...]) + EPS) * gq_ref[...]
    q_ref[...] = (q * HEAD_DIM ** -0.5).astype(BF16)
    c = nq
    kk = proj[:, c:c + 2 * HEAD_DIM]
    kk = kk * lax.rsqrt(_dot_exact(kk * kk, jk_ref[...]) + EPS) * gkk_ref[...]
    kk_ref[...] = kk.astype(BF16)
    c += 2 * HEAD_DIM
    v_ref[...] = proj[:, c:c + DSA_V_DIM].astype(BF16)
    c += DSA_V_DIM
    qi_ref[...] = (proj[:, c:c + IDX_HEADS * IDX_DIM] * IDX_DIM ** -0.5).astype(BF16)
    c += IDX_HEADS * IDX_DIM
    xq_ref[...] = proj[:, c:c + X_WIDTH]
    c += X_WIDTH
    wi_ref[...] = proj[:, c:c + LANES] * IDX_HEADS ** -0.5


def _dsa_in(x2, g, w_in, q_g, k_g, n_heads):
    N, D = x2.shape
    nq = n_heads * HEAD_DIM
    sizes = (nq, HEAD_DIM, DSA_V_DIM, IDX_HEADS * IDX_DIM, IDX_DIM, IDX_HEADS, X_WIDTH)
    o = np.concatenate([[0], np.cumsum(sizes)])
    q_w, k_w, v_w, qi_w, ki_w, wi_w, xq_w = (w_in[:, o[i]:o[i + 1]] for i in range(7))
    w = jnp.concatenate([q_w, k_w, ki_w, v_w, qi_w, xq_w, wi_w,
                         jnp.zeros((D, LANES - IDX_HEADS), w_in.dtype)], axis=1).astype(BF16)
    gq = jnp.tile(q_g, n_heads)[None, :]
    gkk = jnp.concatenate([k_g, jnp.ones((IDX_DIM,), F32)])[None, :]
    tm = min(TOKEN_TILE, N)
    row = lambda c: pl.BlockSpec((tm, c), lambda i: (i, 0))
    widths = (nq, 2 * HEAD_DIM, DSA_V_DIM, IDX_HEADS * IDX_DIM, X_WIDTH, LANES)
    dtypes = (BF16, BF16, BF16, BF16, F32, F32)
    return pl.pallas_call(
        _dsa_in_kernel,
        grid=(N // tm,),
        in_specs=[row(D), _full((1, D)), _full(w.shape), _full((1, nq)), _full((1, 2 * HEAD_DIM)),
                  _full((nq, nq)), _full((2 * HEAD_DIM, 2 * HEAD_DIM))],
        out_specs=[row(c) for c in widths],
        out_shape=[jax.ShapeDtypeStruct((N, c), dt) for c, dt in zip(widths, dtypes)],
        compiler_params=_cparams("parallel"),
        name="dsa_in_proj",
    )(x2, g[None, :], w, gq, gkk, _head_ones(nq, HEAD_DIM), _head_ones(2 * HEAD_DIM, HEAD_DIM))


def _rel_bias_tiles(rel_bias, n_heads):
    max_exact = REL_BUCKETS // 2
    d = np.arange(2 * Q_TILE)
    large = max_exact + (np.log(np.maximum(d, 1).astype(np.float32) / max_exact)
                         / math.log(REL_MAX_DIST / max_exact) * (REL_BUCKETS - max_exact)).astype(np.int32)
    bucket = np.where(d < max_exact, d, np.minimum(large, REL_BUCKETS - 1))
    assert (bucket[Q_TILE:] == REL_BUCKETS - 1).all()
    qi = np.arange(Q_TILE)[:, None]
    ki = np.arange(Q_TILE)[None, :]
    table = (rel_bias - rel_bias[REL_BUCKETS - 1][None, :]).T
    diag = table[:, bucket[np.maximum(qi - ki, 0)]]
    prev = table[:, bucket[qi - ki + Q_TILE]]
    far = jnp.zeros_like(diag)
    return jnp.stack([far, prev, diag]).reshape(3, n_heads * Q_TILE, Q_TILE)


def _dsa_kernel(qs_ref, qi_ref, wi_ref, kt_ref, kit_ref, v_ref, bias_ref, wuv_ref, tri_ref,
                y_ref, keys_ref, madd_ref, m_ref, l_ref, acc_ref, *, topk, n_heads):
    j = pl.program_id(1)
    W = KEY_CHUNK
    n_ch = (j + 2) // 2
    qpos = j * Q_TILE + lax.broadcasted_iota(jnp.int32, (Q_TILE, W), 0)
    lane = lax.broadcasted_iota(jnp.int32, (Q_TILE, W), 1)

    wi = wi_ref[0, 0]
    wcol = [wi[:, h:h + 1] for h in range(IDX_HEADS)]
    qi = qi_ref[0, 0]

    def score_body(c, carry):
        off = pl.multiple_of(c * W, W)
        dots = jnp.maximum(_dot(qi, kit_ref[0, :, pl.ds(off, W)]), 0.0)
        sc = wcol[0] * dots[0:Q_TILE]
        for h in range(1, IDX_HEADS):
            sc = sc + wcol[h] * dots[h * Q_TILE:(h + 1) * Q_TILE]
        sc = jnp.where(off + lane <= qpos, sc, -jnp.inf)
        bits = pltpu.bitcast(sc, jnp.int32)
        keys_ref[:, pl.ds(off, W)] = jnp.where(bits >= 0, bits, bits ^ jnp.int32(0x7FFFFFFF))
        return carry

    lax.fori_loop(0, n_ch, score_body, 0)

    def count(pred_fn):
        def body(c, acc):
            off = pl.multiple_of(c * W, W)
            kk = keys_ref[:, pl.ds(off, W)]
            hit = jnp.where(pred_fn(kk), 1.0, 0.0)
            for t in range(W // LANES):
                acc = acc + hit[:, t * LANES:(t + 1) * LANES]
            return acc
        acc = lax.fori_loop(0, n_ch, body, jnp.zeros((Q_TILE, LANES), F32))
        return jnp.sum(acc, axis=1, keepdims=True)

    kf = float(topk)
    thr = jnp.full((Q_TILE, 1), INT_MIN, jnp.int32)
    zero = jnp.zeros((Q_TILE, 1), jnp.int32)
    thr = jnp.where(count(lambda kk: kk >= zero) >= kf, zero, thr)

    def bit_body(it, thr):
        cand = thr + jnp.left_shift(jnp.int32(1), 30 - it)
        return jnp.where(count(lambda kk: kk >= cand) >= kf, cand, thr)

    thr = lax.fori_loop(0, 31, bit_body, thr)

    need = kf - count(lambda kk: kk > thr)
    tri = tri_ref[...]

    def mask_body(c, run):
        off = pl.multiple_of(c * W, W)
        kk = keys_ref[:, pl.ds(off, W)]
        causal = off + lane <= qpos
        for t in range(W // LANES):
            kt = kk[:, t * LANES:(t + 1) * LANES]
            eq = jnp.where(kt == thr, 1.0, 0.0)
            pre = _dot(eq.astype(BF16), tri) + run
            sel = ((kt > thr) | ((kt == thr) & (pre <= need))) & causal[:, t * LANES:(t + 1) * LANES]
            madd_ref[:, pl.ds(off + t * LANES, LANES)] = jnp.where(sel, 0.0, MASK_NEG)
            run = run + jnp.sum(eq, axis=1, keepdims=True)
        return run

    lax.fori_loop(0, n_ch, mask_body, jnp.zeros((Q_TILE, 1), F32))

    m_ref[...] = jnp.full_like(m_ref, MASK_NEG)
    l_ref[...] = jnp.zeros_like(l_ref)
    acc_ref[...] = jnp.zeros_like(acc_ref)

    def attend(c, with_bias):
        off = pl.multiple_of(c * W, W)
        kc = kt_ref[0, :, pl.ds(off, W)]
        vc = v_ref[0, pl.ds(off, W), :]
        madd = madd_ref[:, pl.ds(off, W)]
        for h in range(n_heads):
            rows = slice(h * Q_TILE, (h + 1) * Q_TILE)
            s = _dot(qs_ref[0, 0, rows, :], kc) + madd
            if with_bias:
                parts = []
                for t in range(W // LANES):
                    kind = jnp.clip(2 * c + t - (j - 2), 0, 2)
                    parts.append(bias_ref[kind, rows, :])
                s = s + jnp.concatenate(parts, axis=1)
            m_old = m_ref[rows, :]
            m_new = jnp.maximum(m_old, jnp.max(s, axis=1, keepdims=True))
            alpha = jnp.exp(m_old - m_new)
            p = jnp.exp(s - m_new)
            l_ref[rows, :] = alpha * l_ref[rows, :] + jnp.sum(p, axis=1, keepdims=True)
            acc_ref[rows, :] = alpha * acc_ref[rows, :] + _dot(p.astype(BF16), vc)
            m_ref[rows, :] = m_new

    n_far = jnp.maximum(n_ch - 2, 0)

    def far_body(c, carry):
        attend(c, False)
        return carry

    def near_body(c, carry):
        attend(c, True)
        return carry

    lax.fori_loop(0, n_far, far_body, 0)
    lax.fori_loop(n_far, n_ch, near_body, 0)

    out = jnp.zeros((Q_TILE, n_heads * HEAD_DIM), F32)
    for h in range(n_heads):
        rows = slice(h * Q_TILE, (h + 1) * Q_TILE)
        o = acc_ref[rows, :] / l_ref[rows, :]
        out = out + _dot(o.astype(BF16), wuv_ref[h])
    y_ref[0] = out


def _dsa_core(qn, kk, v, qi, wi, w_uv, rel_bias, B, S):
    n_heads = w_uv.shape[0]
    nb = S // Q_TILE
    topk = min(TOPK_MAX, S // 4)
    qs = qn.reshape(B, nb, Q_TILE, n_heads, HEAD_DIM).transpose(0, 1, 3, 2, 4).reshape(B, nb, n_heads * Q_TILE, HEAD_DIM)
    qis = qi.reshape(B, nb, Q_TILE, IDX_HEADS, IDX_DIM).transpose(0, 1, 3, 2, 4).reshape(B, nb, IDX_HEADS * Q_TILE, IDX_DIM)
    kk3 = kk.reshape(B, S, 2 * HEAD_DIM)
    kt = kk3[:, :, :HEAD_DIM].transpose(0, 2, 1)
    kit = kk3[:, :, HEAD_DIM:].transpose(0, 2, 1)
    eye = jnp.eye(n_heads, dtype=F32)
    wuvp = jnp.einsum('hcd,hg->hcgd', w_uv, eye).reshape(n_heads, DSA_V_DIM, n_heads * HEAD_DIM).astype(BF16)
    tri = jnp.asarray(np.triu(np.ones((LANES, LANES), np.float32))).astype(BF16)
    bias = _rel_bias_tiles(rel_bias, n_heads)
    hq = n_heads * Q_TILE
    kern = functools.partial(_dsa_kernel, topk=topk, n_heads=n_heads)
    return pl.pallas_call(
        kern,
        grid=(B, nb),
        in_specs=[pl.BlockSpec((1, 1, hq, HEAD_DIM), lambda b, j: (b, j, 0, 0)),
                  pl.BlockSpec((1, 1, IDX_HEADS * Q_TILE, IDX_DIM), lambda b, j: (b, j, 0, 0)),
                  pl.BlockSpec((1, 1, Q_TILE, LANES), lambda b, j: (b, j, 0, 0)),
                  pl.BlockSpec((1, HEAD_DIM, S), lambda b, j: (b, 0, 0)),
                  pl.BlockSpec((1, IDX_DIM, S), lambda b, j: (b, 0, 0)),
                  pl.BlockSpec((1, S, DSA_V_DIM), lambda b, j: (b, 0, 0)),
                  _full(bias.shape), _full(wuvp.shape), _full((LANES, LANES))],
        out_specs=pl.BlockSpec((1, Q_TILE, n_heads * HEAD_DIM), lambda b, j: (b, j, 0)),
        out_shape=jax.ShapeDtypeStruct((B, S, n_heads * HEAD_DIM), F32),
        scratch_shapes=[pltpu.VMEM((Q_TILE, S), jnp.int32), pltpu.VMEM((Q_TILE, S), F32),
                        pltpu.VMEM((hq, 1), F32), pltpu.VMEM((hq, 1), F32), pltpu.VMEM((hq, DSA_V_DIM), F32)],
        compiler_params=_cparams("arbitrary", "arbitrary"),
        name="dsa_core",
    )(qs, qis, wi.reshape(B, nb, Q_TILE, LANES), kt, kit, v.reshape(B, S, DSA_V_DIM), bias, wuvp, tri)


def _mix_out_kernel(x_ref, y_ref, xq_ref, kbd_ref, vbd_ref, gq_ref, j_ref, wo_ref, gf_ref, wr_ref, br_ref,
                    xn_ref, hn_ref, eid_ref, gate_ref):
    n_mem = kbd_ref.shape[2] // X_HEADS
    q = xq_ref[...]
    q = q * lax.rsqrt(_dot_exact(q * q, j_ref[...]) + EPS) * gq_ref[...] * HEAD_DIM ** -0.5
    logits = _dot(q.astype(BF16), kbd_ref[0])
    ps = []
    for h in range(X_HEADS):
        seg = logits[:, h * n_mem:(h + 1) * n_mem]
        e = jnp.exp(seg - jnp.max(seg, axis=1, keepdims=True))
        ps.append(e / jnp.sum(e, axis=1, keepdims=True))
    xo = _dot(jnp.concatenate(ps, axis=1).astype(BF16), vbd_ref[0])
    wd = y_ref.shape[1]
    xn = x_ref[...] + _dot(y_ref[...].astype(BF16), wo_ref[:wd, :]) + _dot(xo.astype(BF16), wo_ref[wd:, :])
    xn_ref[...] = xn
    hn = _rms_rows(xn, gf_ref[...])
    hn_ref[...] = hn

    lt = lax.dot_general(wr_ref[...], hn, (((1,), (1,)), ((), ())), preferred_element_type=F32,
                         precision=HIGHEST) + br_ref[...]
    tm = lt.shape[1]
    n_e = N_GROUPS * EXPERTS_PER_GROUP
    lg = lt[n_e:n_e + SUBLANES, :]
    gmax = jnp.max(lg, axis=0, keepdims=True)
    p_g = 1.0 / jnp.sum(jnp.exp(lg - gmax), axis=0, keepdims=True)
    srow = lax.broadcasted_iota(jnp.int32, (SUBLANES, tm), 0)
    rowf = srow.astype(F32)
    big = float(SUBLANES)
    g_sel = jnp.min(jnp.where(lg == gmax, rowf, big), axis=0, keepdims=True)
    le = jnp.zeros((EXPERTS_PER_GROUP, tm), F32)
    for g in range(N_GROUPS):
        le = jnp.where(g_sel == float(g), lt[g * EXPERTS_PER_GROUP:(g + 1) * EXPERTS_PER_GROUP, :], le)
    ee = jnp.exp(le - jnp.max(le, axis=0, keepdims=True))
    pe = ee / jnp.sum(ee, axis=0, keepdims=True)
    v1 = jnp.max(pe, axis=0, keepdims=True)
    i1 = jnp.min(jnp.where(pe == v1, rowf, big), axis=0, keepdims=True)
    rest = jnp.where(rowf == i1, -1.0, pe)
    v2 = jnp.max(rest, axis=0, keepdims=True)
    i2 = jnp.min(jnp.where(rest == v2, rowf, big), axis=0, keepdims=True)
    base = g_sel * float(EXPERTS_PER_GROUP)
    eid_ref[...] = jnp.where(srow == 0, base + i1, jnp.where(srow == 1, base + i2, 0.0)).astype(jnp.int32)
    den = v1 + v2
    gate_ref[...] = jnp.where(srow == 0, p_g * v1 / den, jnp.where(srow == 1, p_g * v2 / den, 0.0))


def _mix_out(x2, y2, xq2, kbd, vbd, xq_g, w_out, g_ffn, w_grp, b_grp, w_rt, b_rt, S):
    N, D = x2.shape
    wd = y2.shape[1]
    tm = min(TOKEN_TILE, S)
    per_b = S // tm
    n_e = N_GROUPS * EXPERTS_PER_GROUP
    pad = ROUTER_ROWS - n_e - N_GROUPS
    wr = jnp.concatenate([w_rt.T, w_grp.T, jnp.zeros((pad, D), F32)], axis=0)
    br = jnp.concatenate([b_rt, b_grp, jnp.full((pad,), MASK_NEG, F32)])[:, None]
    row = lambda c: pl.BlockSpec((tm, c), lambda i: (i, 0))
    col = pl.BlockSpec((SUBLANES, tm), lambda i: (0, i))
    return pl.pallas_call(
        _mix_out_kernel,
        grid=(N // tm,),
        in_specs=[row(D), row(wd), row(X_WIDTH),
                  pl.BlockSpec((1,) + kbd.shape[1:], lambda i: (i // per_b, 0, 0)),
                  pl.BlockSpec((1,) + vbd.shape[1:], lambda i: (i // per_b, 0, 0)),
                  _full((1, X_WIDTH)), _full((X_WIDTH, X_WIDTH)), _full(w_out.shape), _full((1, D)),
                  _full((ROUTER_ROWS, D)), _full((ROUTER_ROWS, 1))],
        out_specs=[row(D), row(D), col, col],
        out_shape=[jax.ShapeDtypeStruct((N, D), F32), jax.ShapeDtypeStruct((N, D), F32),
                   jax.ShapeDtypeStruct((SUBLANES, N), jnp.int32), jax.ShapeDtypeStruct((SUBLANES, N), F32)],
        compiler_params=_cparams("parallel"),
        name="mix_out_router",
    )(x2, y2, xq2, kbd, vbd, jnp.tile(xq_g, X_HEADS)[None, :], _head_ones(X_WIDTH, HEAD_DIM),
      w_out.astype(BF16), g_ffn[None, :], wr, br)


def _moe_kernel(beid_ref, src_ref, srcn_ref, x_hbm, wg_ref, wu_ref, wd_ref, o_ref, xbuf, sem):
    i = pl.program_id(0)
    nblk = pl.num_programs(0)
    rows = xbuf.shape[1]
    slot = i % 2

    def row_copy(idx_ref, r, s):
        return pltpu.make_async_copy(x_hbm.at[pl.ds(idx_ref[0, 0, r], 1), :], xbuf.at[s, pl.ds(r, 1), :], sem.at[s])

    def start_all(idx_ref, s):
        def body(r, c):
            row_copy(idx_ref, r, s).start()
            return c
        lax.fori_loop(0, rows, body, 0)

    @pl.when(i == 0)
    def _():
        start_all(src_ref, 0)

    @pl.when(i + 1 < nblk)
    def _():
        start_all(srcn_ref, 1 - slot)

    def wait_body(r, c):
        row_copy(src_ref, r, slot).wait()
        return c
    lax.fori_loop(0, rows, wait_body, 0)

    xb = xbuf[slot].astype(BF16)
    g = _dot(xb, wg_ref[0])
    u = _dot(xb, wu_ref[0])
    hmid = (g * jax.nn.sigmoid(g)) * u
    o_ref[...] = _dot(hmid.astype(BF16), wd_ref[0])


def _moe_experts(hn, block_eid, src3, w_gate, w_up, w_down):
    N, D = hn.shape
    nblk = src3.shape[0]
    rows = src3.shape[2]
    F = w_gate.shape[2]
    smem_blk = lambda f: pl.BlockSpec((1, 1, rows), f, memory_space=pltpu.SMEM)
    grid_spec = pltpu.PrefetchScalarGridSpec(
        num_scalar_prefetch=1,
        grid=(nblk,),
        in_specs=[smem_blk(lambda i, be: (i, 0, 0)),
                  smem_blk(lambda i, be: (jnp.minimum(i + 1, nblk - 1), 0, 0)),
                  pl.BlockSpec(memory_space=pl.ANY),
                  pl.BlockSpec((1, D, F), lambda i, be: (be[i], 0, 0)),
                  pl.BlockSpec((1, D, F), lambda i, be: (be[i], 0, 0)),
                  pl.BlockSpec((1, F, D), lambda i, be: (be[i], 0, 0))],
        out_specs=pl.BlockSpec((rows, D), lambda i, be: (i, 0)),
        scratch_shapes=[pltpu.VMEM((2, rows, D), F32), pltpu.SemaphoreType.DMA((2,))],
    )
    return pl.pallas_call(
        _moe_kernel,
        grid_spec=grid_spec,
        out_shape=jax.ShapeDtypeStruct((nblk * rows, D), F32),
        compiler_params=_cparams("arbitrary"),
        name="moe_experts",
    )(block_eid, src3, src3, hn, w_gate.astype(BF16), w_up.astype(BF16), w_down.astype(BF16))


def _combine_kernel(d0_ref, d1_ref, x_ref, g_ref, y_hbm, o_ref, ybuf, sem):
    tm = x_ref.shape[0]

    def row_copy(idx_ref, r, k):
        return pltpu.make_async_copy(y_hbm.at[pl.ds(idx_ref[0, 0, r], 1), :], ybuf.at[k, pl.ds(r, 1), :], sem.at[k])

    def start_body(r, c):
        row_copy(d0_ref, r, 0).start()
        row_copy(d1_ref, r, 1).start()
        return c
    lax.fori_loop(0, tm, start_body, 0)

    def wait_body(r, c):
        row_copy(d0_ref, r, 0).wait()
        row_copy(d1_ref, r, 1).wait()
        return c
    lax.fori_loop(0, tm, wait_body, 0)

    g = g_ref[...]
    o_ref[...] = x_ref[...] + (ybuf[0] * g[:, 0:1] + ybuf[1] * g[:, 1:2])


def _moe_combine(xn, gate_rows, dest, y_sorted):
    N, D = xn.shape
    tm = min(TOKEN_TILE, N)
    nt = N // tm
    d0 = dest[0].reshape(nt, 1, tm)
    d1 = dest[1].reshape(nt, 1, tm)
    gcols = gate_rows.T
    smem_blk = pl.BlockSpec((1, 1, tm), lambda i: (i, 0, 0), memory_space=pltpu.SMEM)
    return pl.pallas_call(
        _combine_kernel,
        grid=(nt,),
        in_specs=[smem_blk, smem_blk, pl.BlockSpec((tm, D), lambda i: (i, 0)),
                  pl.BlockSpec((tm, SUBLANES), lambda i: (i, 0)), pl.BlockSpec(memory_space=pl.ANY)],
        out_specs=pl.BlockSpec((tm, D), lambda i: (i, 0)),
        out_shape=jax.ShapeDtypeStruct((N, D), F32),
        scratch_shapes=[pltpu.VMEM((2, tm, D), F32), pltpu.SemaphoreType.DMA((2,))],
        compiler_params=_cparams("arbitrary"),
        name="moe_combine",
    )(d0, d1, xn, gcols, y_sorted)


def _route_tables(eid_rows, n_tokens):
    n_e = N_GROUPS * EXPERTS_PER_GROUP
    eid = eid_rows[:TOP_K_IN_GROUP]
    flat = eid.reshape(-1)
    onehot = (flat[:, None] == jnp.arange(n_e, dtype=jnp.int32)[None, :]).astype(jnp.int32)
    csum = jnp.cumsum(onehot, axis=0)
    counts = csum[-1]
    rank = jnp.sum((csum - onehot) * onehot, axis=1)
    padded = (counts + MOE_ROWS - 1) // MOE_ROWS * MOE_ROWS
    seg_end = jnp.cumsum(padded)
    pad_start = seg_end - padded
    dest = (pad_start[flat] + rank).astype(jnp.int32)
    nblk = -(-(TOP_K_IN_GROUP * n_tokens) // MOE_ROWS) + n_e
    tok = jnp.tile(jnp.arange(n_tokens, dtype=jnp.int32), TOP_K_IN_GROUP)
    src = jnp.zeros((nblk * MOE_ROWS,), jnp.int32).at[dest].set(tok)
    block_eid = jnp.minimum(jnp.searchsorted(seg_end, jnp.arange(nblk, dtype=jnp.int32) * MOE_ROWS, side='right'),
                            n_e - 1).astype(jnp.int32)
    return block_eid, src.reshape(nblk, 1, MOE_ROWS), dest.reshape(TOP_K_IN_GROUP, n_tokens)


def _hier_moe(xn, hn, eid_rows, gate_rows, w_gate, w_up, w_down):
    block_eid, src3, dest = _route_tables(eid_rows, xn.shape[0])
    y_sorted = _moe_experts(hn, block_eid, src3, w_gate, w_up, w_down)
    return _moe_combine(xn, gate_rows, dest, y_sorted)


def kernel(x, mem, norm_mix, norm_ffn, mem_norm, w_mem_kv, xq_norm, xk_norm, w_out, lru_w_in, lru_conv_w, lru_conv_b, lru_w_a, lru_b_a, lru_w_x, lru_b_x, lru_lambda, dsa_w_in, dsa_q_norm, dsa_k_norm, dsa_w_uv, rel_bias, moe_w_group, moe_b_group, moe_w_router, moe_b_router, moe_w_gate, moe_w_up, moe_w_down):
    B, S, D = x.shape
    depth = norm_mix.shape[0]
    x2 = x.reshape(B * S, D)
    for i in range(depth):
        jt = i // 2
        kbd, vbd = _memory_kv(mem, mem_norm[i], w_mem_kv[i], xk_norm[i])
        if i % 2 == 0:
            xs, gate, xq = _lru_in(x2, norm_mix[i], lru_w_in[jt])
            wd = xs.shape[1]
            y = _lru_core(xs.reshape(B, S, wd), gate.reshape(B, S, wd), lru_conv_w[jt], lru_conv_b[jt],
                          lru_w_a[jt], lru_b_a[jt], lru_w_x[jt], lru_b_x[jt], lru_lambda[jt])
        else:
            n_heads = dsa_w_uv.shape[1]
            qn, kk, v, qi, xq, wi = _dsa_in(x2, norm_mix[i], dsa_w_in[jt], dsa_q_norm[jt], dsa_k_norm[jt], n_heads)
            y = _dsa_core(qn, kk, v, qi, wi, dsa_w_uv[jt], rel_bias, B, S)
        y2 = y.reshape(B * S, y.shape[-1])
        xn, hn, eid_rows, gate_rows = _mix_out(x2, y2, xq, kbd, vbd, xq_norm[i], w_out[i], norm_ffn[i],
                                               moe_w_group[i], moe_b_group[i], moe_w_router[i], moe_b_router[i], S)
        x2 = _hier_moe(xn, hn, eid_rows, gate_rows, moe_w_gate[i], moe_w_up[i], moe_w_down[i])
    return x2.reshape(B, S, D)
```
